```python
import math
import jax, jax.numpy as jnp
from jax import lax
import numpy as np

D_MODEL = 1024
BATCH = 2
SEQ = 8192
DEPTH = 4

CHUNK = 64
Q_BLOCK = 128
MIX_WIDTH = D_MODEL
SSM_WIDTH = MIX_WIDTH // 2
SSM_GROUP_CH = 16
SSM_GROUPS = SSM_WIDTH // SSM_GROUP_CH
SSM_STATE = 64
ATTN_WIDTH = MIX_WIDTH - SSM_WIDTH
DIFF_HEADS = 4
DIFF_V_DIM = ATTN_WIDTH // DIFF_HEADS
DIFF_QK_DIM = DIFF_V_DIM // 2
QK_WIDTH = DIFF_HEADS * 2 * DIFF_QK_DIM
ROT_DIM = DIFF_QK_DIM // 4
ROT_HALF = ROT_DIM // 2
ROPE_THETA = 500000.0
IN_WIDTH = SSM_WIDTH + 2 * QK_WIDTH + ATTN_WIDTH
FFN_DIM = 2816
CONV_WIDTH = 3
EPS = 1e-6

kernel_name = "hybrid_s5_diffattn_convffn_trunk"


def _rmsnorm(x, g):
    xf = x.astype(jnp.float32)
    y = xf * lax.rsqrt(jnp.mean(xf * xf, axis=-1, keepdims=True) + EPS)
    return y.astype(x.dtype) * g


def _rotary(t, cos, sin):
    cos = cos.astype(t.dtype)
    sin = sin.astype(t.dtype)
    t1 = t[..., :ROT_HALF]
    t2 = t[..., ROT_HALF:ROT_DIM]
    return jnp.concatenate([t1 * cos - t2 * sin, t2 * cos + t1 * sin, t[..., ROT_DIM:]], axis=-1)


def _complex_affine_combine(e1, e2):
    a1r, a1i, b1r, b1i = e1
    a2r, a2i, b2r, b2i = e2
    ar = a2r * a1r - a2i * a1i
    ai = a2r * a1i + a2i * a1r
    br = a2r * b1r - a2i * b1i + b2r
    bi = a2r * b1i + a2i * b1r + b2i
    return (ar, ai, br, bi)


def _s5_group(u, lam_re, lam_im, log_step, b_re, b_im, c_re, c_im, d, w_glu, g_norm):
    bsz, seqlen, _ = u.shape
    f32 = jnp.float32
    uf = u.astype(f32).reshape(bsz, seqlen, SSM_GROUPS, SSM_GROUP_CH)
    lr = lam_re.astype(f32)
    li = lam_im.astype(f32)
    step = jnp.exp(log_step.astype(f32))[:, None]
    mag = jnp.exp(step * lr)
    ar = mag * jnp.cos(step * li)
    ai = mag * jnp.sin(step * li)
    den = lr * lr + li * li
    fr = ((ar - 1.0) * lr + ai * li) / den
    fi = (ai * lr - (ar - 1.0) * li) / den
    bur = jnp.einsum('blgc,gpc->blgp', uf, b_re.astype(f32))
    bui = jnp.einsum('blgc,gpc->blgp', uf, b_im.astype(f32))
    xr0 = fr * bur - fi * bui
    xi0 = fr * bui + fi * bur
    a_r = jnp.broadcast_to(ar, xr0.shape)
    a_i = jnp.broadcast_to(ai, xr0.shape)
    _, _, sr, si = lax.associative_scan(_complex_affine_combine, (a_r, a_i, xr0, xi0), axis=1)
    y = (jnp.einsum('blgp,gcp->blgc', sr, c_re.astype(f32))
         - jnp.einsum('blgp,gcp->blgc', si, c_im.astype(f32))
         + d.astype(f32) * uf)
    y = jax.nn.gelu(y.reshape(bsz, seqlen, SSM_WIDTH)).astype(u.dtype)
    ab = y @ w_glu
    out = ab[..., :SSM_WIDTH] * jax.nn.sigmoid(ab[..., SSM_WIDTH:])
    return _rmsnorm(out, g_norm)


def _diff_attention_group(q, k, v, cos, sin, lam, lam_init, g_subln):
    bsz, seqlen, _ = q.shape
    f32 = jnp.float32
    q = _rotary(q.reshape(bsz, seqlen, DIFF_HEADS, 2, DIFF_QK_DIM), cos, sin)
    k = _rotary(k.reshape(bsz, seqlen, DIFF_HEADS, 2, DIFF_QK_DIM), cos, sin)
    q = q.transpose(0, 2, 3, 1, 4).astype(f32)
    k = k.transpose(0, 2, 3, 1, 4).astype(f32)
    vf = v.reshape(bsz, seqlen, DIFF_HEADS, DIFF_V_DIM).transpose(0, 2, 1, 3).astype(f32)
    n_blocks = seqlen // Q_BLOCK
    q_blocks = jnp.moveaxis(q.reshape(bsz, DIFF_HEADS, 2, n_blocks, Q_BLOCK, DIFF_QK_DIM), 3, 0)
    key_chunk = jnp.arange(seqlen) // CHUNK
    scale = 1.0 / math.sqrt(DIFF_QK_DIM)

    def block(args):
        qb, bi = args
        s = jnp.einsum('bhcqd,bhckd->bhcqk', qb, k) * scale
        q_chunk = (bi * Q_BLOCK + jnp.arange(Q_BLOCK)) // CHUNK
        mask = key_chunk[None, :] <= q_chunk[:, None]
        s = jnp.where(mask, s, -jnp.inf)
        p = jax.nn.softmax(s, axis=-1)
        w = p[:, :, 0] - lam * p[:, :, 1]
        return jnp.einsum('bhqk,bhkd->bhqd', w, vf)

    out = lax.map(block, (q_blocks, jnp.arange(n_blocks)))
    out = out.transpose(1, 0, 3, 2, 4).reshape(bsz, seqlen, DIFF_HEADS, DIFF_V_DIM).astype(v.dtype)
    out = _rmsnorm(out, g_subln) * (1.0 - lam_init)
    return out.reshape(bsz, seqlen, ATTN_WIDTH)


def _conv_gated_ffn(h, w_up, w_conv, b_conv, w_down):
    seqlen = h.shape[1]
    up = h @ w_up
    padded = jnp.pad(up, ((0, 0), (CONV_WIDTH - 1, 0), (0, 0)))
    conv = b_conv + sum(w_conv[j] * padded[:, j:j + seqlen] for j in range(CONV_WIDTH))
    gate = conv[..., :FFN_DIM]
    val = conv[..., FFN_DIM:]
    return (jax.nn.silu(gate) * val) @ w_down


def setup_inputs(seed: int = 0) -> dict:
    key = jax.random.key(seed)
    ks = jax.random.split(key, 26)
    f32 = jnp.float32
    nrm = lambda k, shape, s: jax.random.normal(k, shape, f32) * s
    x = jax.random.normal(ks[0], (BATCH, SEQ, D_MODEL), f32)
    offset = jax.random.randint(ks[1], (BATCH, 1), 0, 4096, dtype=jnp.int32)
    positions = offset + jnp.arange(SEQ, dtype=jnp.int32)[None, :]
    n_idx = jnp.arange(SSM_STATE, dtype=f32)
    ssm_lambda_re = -0.5 + nrm(ks[2], (DEPTH, SSM_GROUPS, SSM_STATE), 0.01)
    ssm_lambda_im = math.pi * n_idx + nrm(ks[3], (DEPTH, SSM_GROUPS, SSM_STATE), 0.01)
    ssm_log_step = jax.random.uniform(ks[4], (DEPTH, SSM_GROUPS), f32, math.log(1e-3), math.log(1e-1))
    return {
        "x": x,
        "positions": positions,
        "norm_mix": 1.0 + nrm(ks[5], (DEPTH, D_MODEL), 0.02),
        "w_in": nrm(ks[6], (DEPTH, D_MODEL, IN_WIDTH), D_MODEL ** -0.5),
        "ssm_lambda_re": ssm_lambda_re,
        "ssm_lambda_im": ssm_lambda_im,
        "ssm_log_step": ssm_log_step,
        "ssm_b_re": nrm(ks[7], (DEPTH, SSM_GROUPS, SSM_STATE, SSM_GROUP_CH), (2 * SSM_GROUP_CH) ** -0.5),
        "ssm_b_im": nrm(ks[8], (DEPTH, SSM_GROUPS, SSM_STATE, SSM_GROUP_CH), (2 * SSM_GROUP_CH) ** -0.5),
        "ssm_c_re": nrm(ks[9], (DEPTH, SSM_GROUPS, SSM_GROUP_CH, SSM_STATE), (2 * SSM_STATE) ** -0.5),
        "ssm_c_im": nrm(ks[10], (DEPTH, SSM_GROUPS, SSM_GROUP_CH, SSM_STATE), (2 * SSM_STATE) ** -0.5),
        "ssm_d": nrm(ks[11], (DEPTH, SSM_GROUPS, SSM_GROUP_CH), 1.0),
        "ssm_w_glu": nrm(ks[12], (DEPTH, SSM_WIDTH, 2 * SSM_WIDTH), SSM_WIDTH ** -0.5),
        "ssm_norm": 1.0 + nrm(ks[13], (DEPTH, SSM_WIDTH), 0.02),
        "lambda_q1": nrm(ks[14], (DEPTH, DIFF_QK_DIM), 0.1),
        "lambda_k1": nrm(ks[15], (DEPTH, DIFF_QK_DIM), 0.1),
        "lambda_q2": nrm(ks[16], (DEPTH, DIFF_QK_DIM), 0.1),
        "lambda_k2": nrm(ks[17], (DEPTH, DIFF_QK_DIM), 0.1),
        "attn_subln": 1.0 + nrm(ks[18], (DEPTH, DIFF_V_DIM), 0.02),
        "w_out": nrm(ks[19], (DEPTH, MIX_WIDTH, D_MODEL), MIX_WIDTH ** -0.5),
        "norm_ffn": 1.0 + nrm(ks[20], (DEPTH, D_MODEL), 0.02),
        "w_up": nrm(ks[21], (DEPTH, D_MODEL, 2 * FFN_DIM), D_MODEL ** -0.5),
        "w_conv": nrm(ks[22], (DEPTH, CONV_WIDTH, 2 * FFN_DIM), CONV_WIDTH ** -0.5),
        "b_conv": nrm(ks[23], (DEPTH, 2 * FFN_DIM), 0.02),
        "w_down": nrm(ks[24], (DEPTH, FFN_DIM, D_MODEL), FFN_DIM ** -0.5),
        "norm_final": 1.0 + nrm(ks[25], (D_MODEL,), 0.02),
    }


def reference(x, positions, norm_mix, w_in, ssm_lambda_re, ssm_lambda_im, ssm_log_step,
              ssm_b_re, ssm_b_im, ssm_c_re, ssm_c_im, ssm_d, ssm_w_glu, ssm_norm,
              lambda_q1, lambda_k1, lambda_q2, lambda_k2, attn_subln, w_out,
              norm_ffn, w_up, w_conv, b_conv, w_down, norm_final):
    f32 = jnp.float32
    inv_freq = ROPE_THETA ** (-(jnp.arange(0, ROT_DIM, 2, dtype=f32) / ROT_DIM))
    ang = positions.astype(f32)[..., None] * inv_freq
    cos = jnp.cos(ang)[:, :, None, None, :]
    sin = jnp.sin(ang)[:, :, None, None, :]
    for i in range(DEPTH):
        lam_init = 0.8 - 0.6 * math.exp(-0.3 * i)
        lam = (jnp.exp(jnp.sum(lambda_q1[i].astype(f32) * lambda_k1[i].astype(f32)))
               - jnp.exp(jnp.sum(lambda_q2[i].astype(f32) * lambda_k2[i].astype(f32)))
               + lam_init)
        h = _rmsnorm(x, norm_mix[i])
        proj = h @ w_in[i]
        u = proj[..., :SSM_WIDTH]
        q = proj[..., SSM_WIDTH:SSM_WIDTH + QK_WIDTH]
        k = proj[..., SSM_WIDTH + QK_WIDTH:SSM_WIDTH + 2 * QK_WIDTH]
        v = proj[..., SSM_WIDTH + 2 * QK_WIDTH:]
        y_ssm = _s5_group(u, ssm_lambda_re[i], ssm_lambda_im[i], ssm_log_step[i],
                          ssm_b_re[i], ssm_b_im[i], ssm_c_re[i], ssm_c_im[i], ssm_d[i],
                          ssm_w_glu[i], ssm_norm[i])
        y_att = _diff_attention_group(q, k, v, cos, sin, lam, lam_init, attn_subln[i])
        x = x + jnp.concatenate([y_ssm, y_att], axis=-1) @ w_out[i]
        h = _rmsnorm(x, norm_ffn[i])
        x = x + _conv_gated_ffn(h, w_up[i], w_conv[i], b_conv[i], w_down[i])
    return _rmsnorm(x, norm_final)
```

```python
import functools
import math

import jax
import jax.numpy as jnp
from jax import lax
from jax.experimental import pallas as pl
from jax.experimental.pallas import tpu as pltpu

F32 = jnp.float32
BF16 = jnp.bfloat16

EPS = 1e-6
CHUNK = 64
SSM_GROUP_CH = 16
SSM_STATE = 64
DIFF_HEADS = 4
DIFF_QK_DIM = 64
HEAD_WIDTH = 2 * DIFF_QK_DIM
ROT_DIM = 16
ROT_HALF = ROT_DIM // 2
ROPE_THETA = 500000.0
CONV_WIDTH = 3

LANES = 128
SUBLANES = 8
VMEM_LIMIT = 56 * 1024 * 1024
NEG_BIG = -1e30

TM_PROJ = 512
T_SSM = 256
TQ_ATT = 512
TK_ATT = 256
TM_FFN = 512
FFN_CHUNK = 256


def _dot(a, b):
    return jnp.dot(a, b, preferred_element_type=F32)


def _dot_nt(a, b):
    return lax.dot_general(a, b, (((1,), (1,)), ((), ())), preferred_element_type=F32)


def _rms(x, g):
    ms = jnp.mean(x * x, axis=-1, keepdims=True)
    return (x * lax.rsqrt(ms + EPS)) * g


def _params(sem):
    return pltpu.CompilerParams(dimension_semantics=sem, vmem_limit_bytes=VMEM_LIMIT)


def _const_spec(shape):
    nd = len(shape)
    return pl.BlockSpec(shape, lambda *_: (0,) * nd, pipeline_mode=pl.Buffered(1))


def _inproj_kernel(x_ref, g_ref, w_ref, cos_ref, sa_ref, sb_ref, u_ref, q_ref, k_ref, v_ref,
                   *, widths):
    hb = _rms(x_ref[...], g_ref[...]).astype(BF16)
    cos = cos_ref[...]
    sa = sa_ref[...]
    sb = sb_ref[...]

    def rotary(t):
        outs = []
        for j in range(t.shape[1] // LANES):
            tj = t[:, LANES * j:LANES * (j + 1)]
            outs.append(tj * cos + pltpu.roll(tj, LANES - ROT_HALF, 1) * sa
                        + pltpu.roll(tj, ROT_HALF, 1) * sb)
        return jnp.concatenate(outs, axis=1)

    su, sq, sk, sv = widths
    o = 0
    u_ref[...] = _dot(hb, w_ref[:, o:o + su])
    o += su
    q_ref[...] = (rotary(_dot(hb, w_ref[:, o:o + sq])) * (1.0 / math.sqrt(DIFF_QK_DIM))).astype(BF16)
    o += sq
    k_ref[...] = rotary(_dot(hb, w_ref[:, o:o + sk])).astype(BF16)
    o += sk
    v_ref[...] = _dot(hb, w_ref[:, o:o + sv]).astype(BF16)


def _inproj(x, g, w_bf, cos_t, sa_t, sb_t, widths):
    B, L, D = x.shape
    tm = TM_PROJ
    su, sq, sk, sv = widths
    row = lambda w: pl.BlockSpec((None, tm, w), lambda b, i: (b, i, 0))
    return pl.pallas_call(
        functools.partial(_inproj_kernel, widths=widths),
        grid=(B, L // tm),
        in_specs=[row(D), _const_spec((1, D)), _const_spec(w_bf.shape),
                  row(LANES), row(LANES), row(LANES)],
        out_specs=[row(su), row(sq), row(sk), row(sv)],
        out_shape=[jax.ShapeDtypeStruct((B, L, su), F32),
                   jax.ShapeDtypeStruct((B, L, sq), BF16),
                   jax.ShapeDtypeStruct((B, L, sk), BF16),
                   jax.ShapeDtypeStruct((B, L, sv), BF16)],
        compiler_params=_params(("parallel", "parallel")),
        name="inproj",
    )(x, g, w_bf, cos_t, sa_t, sb_t)


def _ssm_kernel(u0, u1, u2, u3, bre_ref, bim_ref, cre_ref, cimn_ref, av_ref, d_ref, wglu_ref, gn_ref,
                o0, o1, o2, o3, up_ref, sre_ref, sim_ref, carry_ref, *, seg):
    urefs = (u0, u1, u2, u3)
    orefs = (o0, o1, o2, o3)
    width = up_ref.shape[1]
    nstate = sre_ref.shape[1]
    half_in = width // 2
    half_st = nstate // 2

    @pl.when(pl.program_id(1) == 0)
    def _():
        carry_ref[...] = jnp.zeros_like(carry_ref)

    for j in range(seg):
        for cg in range(4):
            up_ref[SUBLANES * j:SUBLANES * (j + 1), LANES * cg:LANES * (cg + 1)] = (
                urefs[cg][pl.ds(j, SUBLANES, stride=seg), :])

    ub = up_ref[...].astype(BF16)
    for h in range(2):
        lhs = ub[:, half_in * h:half_in * (h + 1)]
        sre_ref[:, half_st * h:half_st * (h + 1)] = _dot(lhs, bre_ref[h])
        sim_ref[:, half_st * h:half_st * (h + 1)] = _dot(lhs, bim_ref[h])

    lane_chunk = 4 * LANES
    row = lax.broadcasted_iota(jnp.int32, (SUBLANES, lane_chunk), 0)
    for lc in range(nstate // lane_chunk):
        sl = slice(lane_chunk * lc, lane_chunk * (lc + 1))
        ar = jnp.broadcast_to(av_ref[0:1, sl], (SUBLANES, lane_chunk))
        ai = jnp.broadcast_to(av_ref[1:2, sl], (SUBLANES, lane_chunk))

        def local_scan(j, st):
            pr, pi = st
            r0 = pl.multiple_of(j * SUBLANES, SUBLANES)
            nr = ar * pr - ai * pi + sre_ref[pl.ds(r0, SUBLANES), sl]
            ni = ar * pi + ai * pr + sim_ref[pl.ds(r0, SUBLANES), sl]
            sre_ref[pl.ds(r0, SUBLANES), sl] = nr
            sim_ref[pl.ds(r0, SUBLANES), sl] = ni
            return nr, ni

        zero = jnp.zeros((SUBLANES, lane_chunk), F32)
        er, ei = lax.fori_loop(0, seg, local_scan, (zero, zero))

        xr = jnp.where(row == 0, carry_ref[0:1, sl], pltpu.roll(er, 1, 0))
        xi = jnp.where(row == 0, carry_ref[1:2, sl], pltpu.roll(ei, 1, 0))
        for n, sh in enumerate((1, 2, 4)):
            mr = av_ref[2 + 2 * n:3 + 2 * n, sl]
            mi = av_ref[3 + 2 * n:4 + 2 * n, sl]
            shr = jnp.where(row >= sh, pltpu.roll(xr, sh, 0), 0.0)
            shi = jnp.where(row >= sh, pltpu.roll(xi, sh, 0), 0.0)
            xr, xi = xr + (mr * shr - mi * shi), xi + (mr * shi + mi * shr)
        mr = av_ref[2:3, sl]
        mi = av_ref[3:4, sl]
        outr = mr * xr - mi * xi + er
        outi = mr * xi + mi * xr + ei
        carry_ref[0:1, sl] = outr[SUBLANES - 1:SUBLANES]
        carry_ref[1:2, sl] = outi[SUBLANES - 1:SUBLANES]

        def fixup(j, st):
            cr, ci = st
            nr = ar * cr - ai * ci
            ni = ar * ci + ai * cr
            r0 = pl.multiple_of(j * SUBLANES, SUBLANES)
            sre_ref[pl.ds(r0, SUBLANES), sl] = sre_ref[pl.ds(r0, SUBLANES), sl] + nr
            sim_ref[pl.ds(r0, SUBLANES), sl] = sim_ref[pl.ds(r0, SUBLANES), sl] + ni
            return nr, ni

        lax.fori_loop(0, seg, fixup, (xr, xi))

    ys = []
    for h in range(2):
        st = slice(half_st * h, half_st * (h + 1))
        ys.append(_dot(sre_ref[:, st].astype(BF16), cre_ref[h])
                  + _dot(sim_ref[:, st].astype(BF16), cimn_ref[h]))
    y = jnp.concatenate(ys, axis=1) + d_ref[...] * up_ref[...]
    y = jax.nn.gelu(y)
    ab = _dot(y.astype(BF16), wglu_ref[...])
    out = _rms(ab[:, :width] * jax.nn.sigmoid(ab[:, width:]), gn_ref[...])
    up_ref[...] = out
    for j in range(seg):
        for cg in range(4):
            orefs[cg][pl.ds(j, SUBLANES, stride=seg), :] = (
                up_ref[SUBLANES * j:SUBLANES * (j + 1), LANES * cg:LANES * (cg + 1)])


def _ssm(u, bre, bim, cre, cimn, avec, d_row, wglu_bf, gn):
    B, L, W = u.shape
    assert W == 4 * LANES
    tc = T_SSM
    seg = tc // SUBLANES
    nstate = avec.shape[1]
    slab = lambda cg: pl.BlockSpec((None, tc, LANES), lambda b, i, cg=cg: (b, i, cg))
    return pl.pallas_call(
        functools.partial(_ssm_kernel, seg=seg),
        grid=(B, L // tc),
        in_specs=[slab(0), slab(1), slab(2), slab(3),
                  _const_spec(bre.shape), _const_spec(bim.shape),
                  _const_spec(cre.shape), _const_spec(cimn.shape),
                  _const_spec(avec.shape), _const_spec(d_row.shape),
                  _const_spec(wglu_bf.shape), _const_spec(gn.shape)],
        out_specs=[pl.BlockSpec((None, tc, LANES), lambda b, i: (b, i, 0))] * 4,
        out_shape=[jax.ShapeDtypeStruct((B, L, LANES), F32)] * 4,
        scratch_shapes=[pltpu.VMEM((tc, W), F32), pltpu.VMEM((tc, nstate), F32),
                        pltpu.VMEM((tc, nstate), F32), pltpu.VMEM((SUBLANES, nstate), F32)],
        compiler_params=_params(("parallel", "arbitrary")),
        name="s5_group",
    )(u, u, u, u, bre, bim, cre, cimn, avec, d_row, wglu_bf, gn)


def _ssm_tables(lam_re, lam_im, log_step, b_re, b_im, c_re, c_im, seg):
    G, P = lam_re.shape
    C = b_re.shape[-1]
    lr = lam_re.astype(F32)
    li = lam_im.astype(F32)
    step = jnp.exp(log_step.astype(F32))[:, None]
    mag = jnp.exp(step * lr)
    ar = mag * jnp.cos(step * li)
    ai = mag * jnp.sin(step * li)
    den = lr * lr + li * li
    fr = ((ar - 1.0) * lr + ai * li) / den
    fi = (ai * lr - (ar - 1.0) * li) / den
    bbr = fr[..., None] * b_re.astype(F32) - fi[..., None] * b_im.astype(F32)
    bbi = fr[..., None] * b_im.astype(F32) + fi[..., None] * b_re.astype(F32)
    gh = G // 2
    eye = jnp.eye(gh, dtype=F32)

    def in_blocks(m):
        m = m.reshape(2, gh, P, C)
        return jnp.einsum('hgpc,gk->hgckp', m, eye).reshape(2, gh * C, gh * P).astype(BF16)

    def out_blocks(m):
        m = m.astype(F32).reshape(2, gh, C, P)
        return jnp.einsum('hgcp,gk->hgpkc', m, eye).reshape(2, gh * P, gh * C).astype(BF16)

    def cpow2(zr, zi, n):
        for _ in range(n):
            zr, zi = zr * zr - zi * zi, 2.0 * zr * zi
        return zr, zi

    assert seg & (seg - 1) == 0
    sr, si = cpow2(ar, ai, int(math.log2(seg)))
    rows = [ar, ai, sr, si]
    for _ in range(2):
        sr, si = cpow2(sr, si, 1)
        rows += [sr, si]
    avec = jnp.stack([r.reshape(-1) for r in rows], axis=0)
    return in_blocks(bbr), in_blocks(bbi), out_blocks(c_re), out_blocks(-c_im), avec


def _attn_kernel(sc_ref, q_ref, k_ref, v_ref, g_ref, o_ref, *, tq, tk):
    i = pl.program_id(2)
    q = q_ref[...]
    lane = lax.broadcasted_iota(jnp.int32, q.shape, 1)
    zero = jnp.zeros_like(q)
    q1 = jnp.where(lane < DIFF_QK_DIM, q, zero)
    q2 = jnp.where(lane >= DIFF_QK_DIM, q, zero)

    def update(s, vblk, m, l, acc):
        mn = jnp.maximum(m, jnp.max(s, axis=-1, keepdims=True))
        p = jnp.exp(s - mn)
        al = jnp.exp(m - mn)
        l = al * l + jnp.sum(p, axis=-1, keepdims=True)
        acc = al * acc + _dot(p.astype(BF16), vblk)
        return mn, l, acc

    def step(k0, st, mask):
        m1, l1, a1, m2, l2, a2 = st
        kblk = k_ref[pl.ds(k0, tk), :]
        vblk = v_ref[pl.ds(k0, tk), :]
        s1 = _dot_nt(q1, kblk)
        s2 = _dot_nt(q2, kblk)
        if mask is not None:
            s1 = jnp.where(mask, s1, NEG_BIG)
            s2 = jnp.where(mask, s2, NEG_BIG)
        m1, l1, a1 = update(s1, vblk, m1, l1, a1)
        m2, l2, a2 = update(s2, vblk, m2, l2, a2)
        return m1, l1, a1, m2, l2, a2

    mi = jnp.full((tq, 1), NEG_BIG, F32)
    li = jnp.zeros((tq, 1), F32)
    ai = jnp.zeros((tq, HEAD_WIDTH), F32)
    st = (mi, li, ai, mi, li, ai)
    nd = tq // tk
    st = lax.fori_loop(0, i * nd, lambda kb, s: step(pl.multiple_of(kb * tk, tk), s, None), st)
    qc = lax.broadcasted_iota(jnp.int32, (tq, tk), 0) // CHUNK
    kc = lax.broadcasted_iota(jnp.int32, (tq, tk), 1) // CHUNK
    for d in range(nd):
        mask = (kc + (d * tk) // CHUNK) <= qc
        st = step(pl.multiple_of(i * tq + d * tk, tk), st, mask)
    m1, l1, a1, m2, l2, a2 = st
    o = a1 / l1 - sc_ref[0] * (a2 / l2)
    o_ref[...] = (_rms(o, g_ref[...]) * sc_ref[1]).astype(o_ref.dtype)


def _attention(scal, q, k, v, g):
    B, L, W = q.shape
    H = W // HEAD_WIDTH
    tq, tk = TQ_ATT, TK_ATT
    qspec = pl.BlockSpec((None, tq, HEAD_WIDTH), lambda b, h, i: (b, i, h))
    kvspec = pl.BlockSpec((None, L, HEAD_WIDTH), lambda b, h, i: (b, 0, h))
    return pl.pallas_call(
        functools.partial(_attn_kernel, tq=tq, tk=tk),
        grid=(B, H, L // tq),
        in_specs=[pl.BlockSpec(memory_space=pltpu.SMEM), qspec, kvspec, kvspec,
                  _const_spec((1, HEAD_WIDTH))],
        out_specs=qspec,
        out_shape=jax.ShapeDtypeStruct((B, L, W), BF16),
        compiler_params=_params(("parallel", "parallel", "arbitrary")),
        name="diff_attention",
    )(scal, q, k, v, g)


def _ffn_kernel(x_ref, s0, s1, s2, s3, att_ref, wout_ref, g_ref, wup_ref, cw_ref, wdn_ref, gfin_ref,
                o_ref, buf_ref, carry_ref, *, final):
    tm = x_ref.shape[0]
    ssm_w = 4 * LANES
    nchunk = wup_ref.shape[0]
    fc = wdn_ref.shape[1]

    @pl.when(pl.program_id(1) == 0)
    def _():
        carry_ref[...] = jnp.zeros_like(carry_ref)

    mix = jnp.concatenate([s0[...], s1[...], s2[...], s3[...]], axis=1).astype(BF16)
    x1 = x_ref[...] + _dot(mix, wout_ref[0:ssm_w, :]) + _dot(att_ref[...], wout_ref[ssm_w:, :])
    hb = _rms(x1, g_ref[...]).astype(BF16)
    acc = jnp.zeros_like(x1)
    for f in range(nchunk):
        up = _dot(hb, wup_ref[f])
        buf_ref[0:SUBLANES, :] = carry_ref[f]
        buf_ref[SUBLANES:SUBLANES + tm, :] = up
        carry_ref[f] = up[tm - SUBLANES:tm, :]
        cw = cw_ref[f]
        conv = cw[3:4] + ((cw[0:1] * buf_ref[SUBLANES - 2:SUBLANES - 2 + tm, :]
                           + cw[1:2] * buf_ref[SUBLANES - 1:SUBLANES - 1 + tm, :])
                          + cw[2:3] * up)
        act = (jax.nn.silu(conv[:, :fc]) * conv[:, fc:]).astype(BF16)
        acc = acc + _dot(act, wdn_ref[f])
    out = x1 + acc
    if final:
        out = _rms(out, gfin_ref[...])
    o_ref[...] = out


def _ffn(x, slabs, att, wout_bf, g, wup_r, cw_r, wdn_r, gfin, final):
    B, L, D = x.shape
    tm = TM_FFN
    nchunk, _, fc2 = wup_r.shape
    row = lambda w: pl.BlockSpec((None, tm, w), lambda b, i: (b, i, 0))
    return pl.pallas_call(
        functools.partial(_ffn_kernel, final=final),
        grid=(B, L // tm),
        in_specs=[row(D), row(LANES), row(LANES), row(LANES), row(LANES), row(att.shape[-1]),
                  _const_spec(wout_bf.shape), _const_spec((1, D)), _const_spec(wup_r.shape),
                  _const_spec(cw_r.shape), _const_spec(wdn_r.shape), _const_spec((1, D))],
        out_specs=row(D),
        out_shape=jax.ShapeDtypeStruct((B, L, D), F32),
        scratch_shapes=[pltpu.VMEM((tm + SUBLANES, fc2), F32),
                        pltpu.VMEM((nchunk, SUBLANES, fc2), F32)],
        compiler_params=_params(("parallel", "arbitrary")),
        name="outproj_ffn",
    )(x, *slabs, att, wout_bf, g, wup_r, cw_r, wdn_r, gfin)


def _ffn_tables(w_up, w_conv, b_conv, w_down):
    D, two_f = w_up.shape
    ffn = two_f // 2
    fc = FFN_CHUNK
    assert ffn % fc == 0
    n = ffn // fc

    def regroup(m):
        gate = m[..., :ffn].reshape(m.shape[:-1] + (n, fc))
        val = m[..., ffn:].reshape(m.shape[:-1] + (n, fc))
        return jnp.moveaxis(jnp.concatenate([gate, val], axis=-1), -2, 0)

    wup_r = regroup(w_up).astype(BF16)
    taps = jnp.concatenate([w_conv, b_conv[None, :],
                            jnp.zeros((SUBLANES - CONV_WIDTH - 1, two_f), F32)], axis=0)
    cw_r = regroup(taps)
    wdn_r = w_down.reshape(n, fc, D).astype(BF16)
    return wup_r, cw_r, wdn_r


def _rope_tables(positions):
    inv_freq = ROPE_THETA ** (-(jnp.arange(0, ROT_DIM, 2, dtype=F32) / ROT_DIM))
    ang = positions.astype(F32)[..., None] * inv_freq
    cos = jnp.cos(ang)
    sin = jnp.sin(ang)
    lead = ang.shape[:-1]
    rest = DIFF_QK_DIM - ROT_DIM
    z8 = jnp.zeros(lead + (ROT_HALF,), F32)
    cos64 = jnp.concatenate([cos, cos, jnp.ones(lead + (rest,), F32)], axis=-1)
    sa64 = jnp.concatenate([-sin, z8, jnp.zeros(lead + (rest,), F32)], axis=-1)
    sb64 = jnp.concatenate([z8, sin, jnp.zeros(lead + (rest,), F32)], axis=-1)
    dup = lambda t: jnp.concatenate([t, t], axis=-1)
    return dup(cos64), dup(sa64), dup(sb64)


def kernel(x, positions, norm_mix, w_in, ssm_lambda_re, ssm_lambda_im, ssm_log_step, ssm_b_re, ssm_b_im, ssm_c_re, ssm_c_im, ssm_d, ssm_w_glu, ssm_norm, lambda_q1, lambda_k1, lambda_q2, lambda_k2, attn_subln, w_out, norm_ffn, w_up, w_conv, b_conv, w_down, norm_final):
    depth = w_in.shape[0]
    D = x.shape[-1]
    ssm_w = ssm_w_glu.shape[1]
    att_w = DIFF_HEADS * HEAD_WIDTH
    widths = (ssm_w, att_w, att_w, att_w)
    assert sum(widths) == w_in.shape[-1]
    cos_t, sa_t, sb_t = _rope_tables(positions)
    seg = T_SSM // SUBLANES
    for i in range(depth):
        lam_init = 0.8 - 0.6 * math.exp(-0.3 * i)
        lam = (jnp.exp(jnp.sum(lambda_q1[i].astype(F32) * lambda_k1[i].astype(F32)))
               - jnp.exp(jnp.sum(lambda_q2[i].astype(F32) * lambda_k2[i].astype(F32)))
               + lam_init)
        scal = jnp.stack([lam, jnp.asarray(1.0 - lam_init, F32)]).astype(F32)

        u, q, k, v = _inproj(x, norm_mix[i][None, :], w_in[i].astype(BF16), cos_t, sa_t, sb_t, widths)
        bre, bim, cre, cimn, avec = _ssm_tables(ssm_lambda_re[i], ssm_lambda_im[i], ssm_log_step[i],
                                                ssm_b_re[i], ssm_b_im[i], ssm_c_re[i], ssm_c_im[i], seg)
        slabs = _ssm(u, bre, bim, cre, cimn, avec, ssm_d[i].reshape(1, ssm_w).astype(F32),
                     ssm_w_glu[i].astype(BF16), ssm_norm[i][None, :])
        att = _attention(scal, q, k, v, attn_subln[i][None, :])
        wup_r, cw_r, wdn_r = _ffn_tables(w_up[i], w_conv[i], b_conv[i], w_down[i])
        x = _ffn(x, slabs, att, w_out[i].astype(BF16), norm_ffn[i][None, :], wup_r, cw_r, wdn_r,
                 norm_final[None, :], final=(i == depth - 1))
    return x
```
